```python
import functools
import jax
import jax.numpy as jnp
from jax import lax
import numpy as np

D_MODEL = 1024
BATCH = 2
SEQ = 8192
DEPTH = 4
DEC_BATCH = 128
DEC_SEQ = 4
PAST_LEN = 2048
PAGE_SIZE = 128

HEAD_DIM = 64
POOL_WIDTH = D_MODEL // 4
POOL_WINDOWS = (2, 4, 8, 16)
POOL_GROUPS = len(POOL_WINDOWS)
POOL_GROUP = POOL_WIDTH // POOL_GROUPS
POOL_BUF = max(POOL_WINDOWS) - 1
NSA_WIDTH = D_MODEL // 2
NSA_HEADS = NSA_WIDTH // HEAD_DIM
NSA_KV_HEADS = NSA_HEADS // 4
NSA_BLOCK = 64
NSA_TOP_K = 16
NSA_WINDOW = 512
NSA_QBLOCK = 128
NSA_FORCE = 1e4
GLA_WIDTH = D_MODEL // 4
GLA_HEADS = 4
GLA_DV = GLA_WIDTH // GLA_HEADS
GLA_DK = GLA_DV // 2
GLA_RANK = 16
GLA_NORMALIZER = 16.0
GLA_CHUNK = 64
D_MIX = POOL_WIDTH + NSA_WIDTH + GLA_WIDTH
D_FF = ((8 * D_MODEL // 3 + 255) // 256) * 256
CONV_WIDTH = 3
ROPE_THETA = 10000.0
EPS = 1e-6
IN_SIZES = (POOL_WIDTH, NSA_HEADS * HEAD_DIM, 6 * NSA_KV_HEADS * HEAD_DIM, 3 * NSA_HEADS,
            GLA_HEADS * GLA_DK, GLA_HEADS * GLA_DK, GLA_HEADS * GLA_DV, GLA_RANK, GLA_WIDTH)
D_IN = sum(IN_SIZES)

kernel_name = 'hybrid_pool_nsa_gla_decode_step'


def rmsnorm(x, g):
    xf = x.astype(jnp.float32)
    y = xf * lax.rsqrt(jnp.mean(xf * xf, axis=-1, keepdims=True) + EPS)
    return (y * g.astype(jnp.float32)).astype(x.dtype)


def rope(x, pos):
    half = x.shape[-1] // 2
    inv = ROPE_THETA ** (-jnp.arange(half, dtype=jnp.float32) / half)
    ang = pos.astype(jnp.float32)[:, None] * inv[None, :]
    shape = (1, pos.shape[0]) + (1,) * (x.ndim - 3) + (half,)
    cos, sin = jnp.cos(ang).reshape(shape), jnp.sin(ang).reshape(shape)
    xf = x.astype(jnp.float32)
    x1, x2 = xf[..., :half], xf[..., half:]
    return jnp.concatenate([x1 * cos - x2 * sin, x2 * cos + x1 * sin], axis=-1)


def masked_softmax(s, mask):
    s = jnp.where(mask, s, -jnp.inf)
    m = jnp.max(s, axis=-1, keepdims=True)
    m = jnp.where(jnp.isfinite(m), m, 0.0)
    p = jnp.where(mask, jnp.exp(s - m), 0.0)
    return p / jnp.maximum(jnp.sum(p, axis=-1, keepdims=True), 1e-30)


def split_cols(z):
    out, o = [], 0
    for n in IN_SIZES:
        out.append(z[..., o:o + n])
        o += n
    return out


def pool_mix(u, buf, start, w, scale):
    B, L, C = u.shape
    P = buf.shape[1]
    full = jnp.concatenate([buf.astype(u.dtype), u], axis=1)
    ff = full.astype(jnp.float32)
    cs = jnp.concatenate([jnp.zeros((B, 1, C), jnp.float32), jnp.cumsum(ff, axis=1)], axis=1)
    pos = start + jnp.arange(L)
    means = []
    for g, win in enumerate(POOL_WINDOWS):
        sl = slice(g * POOL_GROUP, (g + 1) * POOL_GROUP)
        hi = cs[:, P + 1:P + 1 + L, sl]
        lo = cs[:, P + 1 - win:P + 1 - win + L, sl]
        cnt = jnp.minimum(win, pos + 1).astype(jnp.float32)[None, :, None]
        means.append((hi - lo) / cnt)
    d = (jnp.concatenate(means, axis=-1) - u.astype(jnp.float32)).reshape(B, L, POOL_GROUPS, POOL_GROUP)
    y = jnp.einsum('blgc,gcd->blgd', d, w.astype(jnp.float32)).reshape(B, L, C) * scale.astype(jnp.float32)
    return y, full[:, -P:]


def gla_mix(q, k, v, lg, s0):
    B, L, H, _ = q.shape
    C = GLA_CHUNK
    nc = -(-L // C)
    pad = nc * C - L

    def prep(a):
        a = jnp.pad(a.astype(jnp.float32), ((0, 0), (0, pad), (0, 0), (0, 0)))
        return jnp.moveaxis(a.reshape(B, nc, C, H, a.shape[-1]), 1, 0)

    tril = jnp.tril(jnp.ones((C, C), dtype=bool))

    def step(S, inp):
        qc, kc, vc, gc = inp
        b = jnp.cumsum(gc, axis=1)
        qt = qc * jnp.exp(b)
        kt = kc * jnp.exp(-b)
        A = jnp.where(tril, jnp.einsum('bihd,bjhd->bhij', qt, kt), 0.0)
        o = jnp.einsum('bhij,bjhv->bihv', A, vc) + jnp.einsum('bihd,bhdv->bihv', qt, S)
        bl = b[:, -1]
        S = jnp.exp(bl)[..., None] * S + jnp.einsum('bjhd,bjhv->bhdv', kc * jnp.exp(bl[:, None] - b), vc)
        return S, o

    S, o = lax.scan(step, s0.astype(jnp.float32), (prep(q), prep(k), prep(v), prep(lg)))
    o = jnp.moveaxis(o, 0, 1).reshape(B, nc * C, H, v.shape[-1])[:, :L]
    return o, S


def block_means(k, nb):
    B = k.shape[0]
    kb = k[:, :nb * NSA_BLOCK].astype(jnp.float32)
    return kb.reshape(B, nb, NSA_BLOCK, NSA_KV_HEADS, HEAD_DIM).mean(axis=2)


def to_blocks(k, nb):
    B, L = k.shape[:2]
    k = jnp.pad(k, ((0, 0), (0, nb * NSA_BLOCK - L), (0, 0), (0, 0)))
    return k.reshape(B, nb, NSA_BLOCK, NSA_KV_HEADS, HEAD_DIM).transpose(0, 3, 1, 2, 4)


def nsa_core(q, q_pos, kc, vc, k_blk, v_blk, k_w, v_w, w_pos, gates):
    B, Q = q.shape[:2]
    G, HPG = NSA_KV_HEADS, NSA_HEADS // NSA_KV_HEADS
    f32 = jnp.float32
    qg = q.astype(f32).reshape(B, Q, G, HPG, HEAD_DIM) * HEAD_DIM ** -0.5
    nbc = kc.shape[1]
    s_c = jnp.einsum('bqghd,bngd->bqghn', qg, kc.astype(f32))
    ok_c = (jnp.arange(nbc) + 1) * NSA_BLOCK - 1 <= q_pos[:, None]
    p_c = masked_softmax(s_c, ok_c[None, :, None, None, :])
    o_c = jnp.einsum('bqghn,bngd->bqghd', p_c, vc.astype(f32))
    nbs = k_blk.shape[2]
    blk = jnp.arange(nbs)
    cur = q_pos // NSA_BLOCK
    imp = jnp.pad(jnp.sum(p_c, axis=3), ((0, 0), (0, 0), (0, 0), (0, nbs - nbc)))
    forced = (blk[None, :] == 0) | (blk[None, :] == cur[:, None])
    allowed = blk[None, :] <= cur[:, None]
    imp = jnp.where(forced[None, :, None, :], NSA_FORCE, imp)
    imp = jnp.where(allowed[None, :, None, :], imp, -jnp.inf)
    top_v, top_i = lax.top_k(imp, min(NSA_TOP_K, nbs))
    kk = top_i.shape[-1]
    bi = jnp.arange(B)[:, None, None, None]
    gi = jnp.arange(G)[None, None, :, None]
    k_sel = k_blk[bi, gi, top_i].astype(f32).reshape(B, Q, G, kk * NSA_BLOCK, HEAD_DIM)
    v_sel = v_blk[bi, gi, top_i].astype(f32).reshape(B, Q, G, kk * NSA_BLOCK, HEAD_DIM)
    kpos = top_i[..., None] * NSA_BLOCK + jnp.arange(NSA_BLOCK)
    ok_s = ((top_v > -jnp.inf)[..., None] & (kpos <= q_pos[None, :, None, None, None]))
    ok_s = ok_s.reshape(B, Q, G, 1, kk * NSA_BLOCK)
    s_s = jnp.einsum('bqghd,bqgkd->bqghk', qg, k_sel)
    o_s = jnp.einsum('bqghk,bqgkd->bqghd', masked_softmax(s_s, ok_s), v_sel)
    s_w = jnp.einsum('bqghd,blgd->bqghl', qg, k_w.astype(f32))
    dist = q_pos[:, None] - w_pos[None, :]
    ok_w = (dist >= 0) & (dist <= NSA_WINDOW) & (w_pos[None, :] >= 0)
    o_w = jnp.einsum('bqghl,blgd->bqghd', masked_softmax(s_w, ok_w[None, :, None, None, :]), v_w.astype(f32))
    g = gates.reshape(B, Q, 3, G, HPG, 1)
    o = g[:, :, 0] * o_c + g[:, :, 1] * o_s + g[:, :, 2] * o_w
    return o.reshape(B, Q, NSA_HEADS * HEAD_DIM)


def nsa_prompt(q, kv, gates, wb):
    B, L = q.shape[:2]
    nbc = L // NSA_BLOCK
    nbs = -(-L // NSA_BLOCK)
    kc, vc = block_means(kv[:, :, 0], nbc), block_means(kv[:, :, 1], nbc)
    k_blk, v_blk = to_blocks(kv[:, :, 2], nbs), to_blocks(kv[:, :, 3], nbs)
    band = NSA_QBLOCK + NSA_WINDOW
    padw = ((0, 0), (NSA_WINDOW, 0), (0, 0), (0, 0))
    kw, vw = jnp.pad(kv[:, :, 4], padw), jnp.pad(kv[:, :, 5], padw)

    def one_block(i):
        s0 = i * NSA_QBLOCK
        sl = lambda a, n: lax.dynamic_slice_in_dim(a, s0, n, axis=1)
        return nsa_core(sl(q, NSA_QBLOCK), s0 + jnp.arange(NSA_QBLOCK), kc, vc, k_blk, v_blk,
                        sl(kw, band), sl(vw, band), s0 - NSA_WINDOW + jnp.arange(band), sl(gates, NSA_QBLOCK))

    o = lax.map(one_block, jnp.arange(L // NSA_QBLOCK))
    o = jnp.moveaxis(o, 0, 1).reshape(B, L, NSA_HEADS * HEAD_DIM)
    win = jnp.pad(kv[:, :, 4:6], ((0, 0), (wb, 0), (0, 0), (0, 0), (0, 0)))[:, -wb:]
    return o, kv[:, :, 0:4], win


def nsa_sample(q, kv, gates, cache_l, page_table, win_buf):
    B, L = q.shape[:2]
    past = page_table.shape[1] * cache_l.shape[1]
    past_rows = cache_l[page_table].reshape(B, past, 4, NSA_KV_HEADS, HEAD_DIM)
    full = jnp.concatenate([past_rows, kv[:, :, 0:4].astype(cache_l.dtype)], axis=1)
    tot = past + L
    nbc = tot // NSA_BLOCK
    nbs = -(-tot // NSA_BLOCK)
    kc, vc = block_means(full[:, :, 0], nbc), block_means(full[:, :, 1], nbc)
    k_blk, v_blk = to_blocks(full[:, :, 2], nbs), to_blocks(full[:, :, 3], nbs)
    wb = win_buf.shape[1]
    wkv = jnp.concatenate([win_buf.astype(kv.dtype), kv[:, :, 4:6]], axis=1)
    o = nsa_core(q, past + jnp.arange(L), kc, vc, k_blk, v_blk, wkv[:, :, 0], wkv[:, :, 1],
                 past - wb + jnp.arange(wb + L), gates)
    return o, kv[:, :, 0:4], wkv[:, -wb:]


def conv_ffn(h, buf, w_up, conv_w, conv_b, w_down):
    L = h.shape[1]
    up = h @ w_up
    a, b = up[..., :D_FF], up[..., D_FF:]
    full = jnp.concatenate([buf.astype(a.dtype), a], axis=1)
    ac = conv_b + sum(conv_w[j] * full[:, j:j + L] for j in range(CONV_WIDTH))
    mid = jax.nn.gelu(ac.astype(jnp.float32), approximate=True) * b.astype(jnp.float32)
    return mid.astype(h.dtype) @ w_down, full[:, -(CONV_WIDTH - 1):]


def layer(x, start, pool_buf, gla_s0, conv_buf, nsa_fn, g_norm, w_in, w_out, pool_w, pool_scale,
          gla_wa2, gla_ba, gla_norm, w_up, conv_w, conv_b, w_down):
    B, L, _ = x.shape
    pos = start + jnp.arange(L)
    h = rmsnorm(x, g_norm[0])
    z = h @ w_in
    u, nq, nkv, ng, gq, gk, gv, glr, gog = split_cols(z)
    y_pool, pool_new = pool_mix(u, pool_buf, start, pool_w, pool_scale)
    q = rope(nq.reshape(B, L, NSA_HEADS, HEAD_DIM), pos)
    kv6 = nkv.reshape(B, L, 6, NSA_KV_HEADS, HEAD_DIM)
    keys = rope(kv6[:, :, 0::2], pos).astype(z.dtype)
    kv = jnp.stack([keys[:, :, 0], kv6[:, :, 1], keys[:, :, 1], kv6[:, :, 3], keys[:, :, 2], kv6[:, :, 5]], axis=2)
    gates = jax.nn.sigmoid(ng.astype(jnp.float32)).reshape(B, L, 3, NSA_HEADS)
    y_nsa, nsa_rows, nsa_win = nsa_fn(q, kv, gates)
    lg = jax.nn.log_sigmoid((glr @ gla_wa2 + gla_ba).astype(jnp.float32)) / GLA_NORMALIZER
    o_gla, gla_new = gla_mix(gq.reshape(B, L, GLA_HEADS, GLA_DK) * GLA_DK ** -0.5,
                             gk.reshape(B, L, GLA_HEADS, GLA_DK), gv.reshape(B, L, GLA_HEADS, GLA_DV),
                             lg.reshape(B, L, GLA_HEADS, GLA_DK), gla_s0)
    o_gla = rmsnorm(o_gla, gla_norm.reshape(GLA_HEADS, GLA_DV)).reshape(B, L, GLA_WIDTH)
    y_gla = o_gla * jax.nn.silu(gog.astype(jnp.float32))
    mix = jnp.concatenate([y_pool, y_nsa, y_gla], axis=-1).astype(x.dtype)
    x = x + rmsnorm(mix @ w_out, g_norm[1])
    f, conv_new = conv_ffn(rmsnorm(x, g_norm[2]), conv_buf, w_up, conv_w, conv_b, w_down)
    x = x + rmsnorm(f, g_norm[3])
    return x, (nsa_rows, nsa_win, pool_new, gla_new, conv_new)


def setup_inputs(seed: int = 0) -> dict:
    key = jax.random.key(seed)
    ks = jax.random.split(key, 20)
    nrm = jax.random.normal
    n_pages = PAST_LEN // PAGE_SIZE
    n_used = DEC_BATCH * n_pages
    n_phys = n_used + n_used // 4
    wb = min(NSA_WINDOW, PAST_LEN)
    page_table = jax.random.permutation(ks[7], n_phys)[:n_used].reshape(DEC_BATCH, n_pages).astype(jnp.int32)
    return {
        'x_prompt': nrm(ks[0], (BATCH, SEQ, D_MODEL), jnp.float32),
        'x_sample': nrm(ks[1], (DEC_BATCH, DEC_SEQ, D_MODEL), jnp.float32),
        'cache_nsa_kv': nrm(ks[2], (DEPTH, n_phys, PAGE_SIZE, 4, NSA_KV_HEADS, HEAD_DIM), jnp.float32),
        'state_nsa_win': nrm(ks[3], (DEPTH, DEC_BATCH, wb, 2, NSA_KV_HEADS, HEAD_DIM), jnp.float32),
        'state_pool': nrm(ks[4], (DEPTH, DEC_BATCH, POOL_BUF, POOL_WIDTH), jnp.float32),
        'state_gla': 0.5 * nrm(ks[5], (DEPTH, DEC_BATCH, GLA_HEADS, GLA_DK, GLA_DV), jnp.float32),
        'state_ffn_conv': nrm(ks[6], (DEPTH, DEC_BATCH, CONV_WIDTH - 1, D_FF), jnp.float32),
        'page_table': page_table,
        'norm_gains': 1.0 + 0.05 * nrm(ks[8], (DEPTH, 4, D_MODEL), jnp.float32),
        'w_in': nrm(ks[9], (DEPTH, D_MODEL, D_IN), jnp.float32) * D_MODEL ** -0.5,
        'w_out': nrm(ks[10], (DEPTH, D_MIX, D_MODEL), jnp.float32) * D_MIX ** -0.5,
        'pool_w': nrm(ks[11], (DEPTH, POOL_GROUPS, POOL_GROUP, POOL_GROUP), jnp.float32) * POOL_GROUP ** -0.5,
        'pool_scale': 1.0 + 0.1 * nrm(ks[12], (DEPTH, POOL_WIDTH), jnp.float32),
        'gla_wa2': nrm(ks[13], (DEPTH, GLA_RANK, GLA_HEADS * GLA_DK), jnp.float32) * GLA_RANK ** -0.5,
        'gla_ba': 0.1 * nrm(ks[14], (DEPTH, GLA_HEADS * GLA_DK), jnp.float32),
        'gla_norm': 1.0 + 0.05 * nrm(ks[15], (DEPTH, GLA_WIDTH), jnp.float32),
        'w_up': nrm(ks[16], (DEPTH, D_MODEL, 2 * D_FF), jnp.float32) * D_MODEL ** -0.5,
        'conv_w': nrm(ks[17], (DEPTH, CONV_WIDTH, D_FF), jnp.float32) * CONV_WIDTH ** -0.5,
        'conv_b': 0.01 * nrm(ks[18], (DEPTH, D_FF), jnp.float32),
        'w_down': nrm(ks[19], (DEPTH, D_FF, D_MODEL), jnp.float32) * D_FF ** -0.5,
    }


def reference(x_prompt, x_sample, cache_nsa_kv, state_nsa_win, state_pool, state_gla, state_ffn_conv, page_table,
              norm_gains, w_in, w_out, pool_w, pool_scale, gla_wa2, gla_ba, gla_norm, w_up, conv_w, conv_b, w_down):
    B = x_prompt.shape[0]
    wb = state_nsa_win.shape[2]
    past = page_table.shape[1] * cache_nsa_kv.shape[2]
    zero_pool = jnp.zeros((B, POOL_BUF, POOL_WIDTH), x_prompt.dtype)
    zero_gla = jnp.zeros((B, GLA_HEADS, GLA_DK, GLA_DV), jnp.float32)
    zero_conv = jnp.zeros((B, CONV_WIDTH - 1, D_FF), x_prompt.dtype)
    yp, ys = x_prompt, x_sample
    sp, ss = [[] for _ in range(5)], [[] for _ in range(5)]
    for l in range(DEPTH):
        weights = (norm_gains[l], w_in[l], w_out[l], pool_w[l], pool_scale[l], gla_wa2[l], gla_ba[l],
                   gla_norm[l], w_up[l], conv_w[l], conv_b[l], w_down[l])
        yp, st_p = layer(yp, 0, zero_pool, zero_gla, zero_conv,
                         functools.partial(nsa_prompt, wb=wb), *weights)
        ys, st_s = layer(ys, past, state_pool[l], state_gla[l], state_ffn_conv[l],
                         functools.partial(nsa_sample, cache_l=cache_nsa_kv[l], page_table=page_table,
                                           win_buf=state_nsa_win[l]), *weights)
        for i in range(5):
            sp[i].append(st_p[i])
            ss[i].append(st_s[i])
    p_rows, p_win, p_pool, p_gla, p_conv = [jnp.stack(a) for a in sp]
    s_rows, s_win, s_pool, s_gla, s_conv = [jnp.stack(a) for a in ss]
    return (yp, ys, p_rows, s_rows, p_win, s_win, p_pool, s_pool, p_gla, s_gla, p_conv, s_conv)
```

```python
import functools

import numpy as np
import jax
import jax.numpy as jnp
from jax import lax
from jax.experimental import pallas as pl
from jax.experimental.pallas import tpu as pltpu

F32 = jnp.float32
BF16 = jnp.bfloat16
NEG_INF = float("-inf")
HIGHEST = lax.Precision.HIGHEST

D_MODEL = 1024
HEAD_DIM = 64
POOL_WIDTH = 256
POOL_WINDOWS = (2, 4, 8, 16)
POOL_BUF = 15
NSA_WIDTH = 512
NSA_HEADS = 8
NSA_KV_HEADS = 2
NSA_BLOCK = 64
NSA_TOP_K = 16
NSA_WINDOW = 512
NSA_FORCE = 1e4
GLA_WIDTH = 256
GLA_HEADS = 4
GLA_DK = 32
GLA_DV = 64
GLA_RANK = 16
GLA_NORMALIZER = 16.0
GLA_CHUNK = 64
CONV_WIDTH = 3
ROPE_THETA = 10000.0
EPS = 1e-6

LANES = 128
SUBLANES = 8

OFF_U, OFF_Q, OFF_KV, OFF_NG, OFF_GQ, OFF_GK, OFF_GV, OFF_LR, OFF_OG, D_INP = (
    0, 256, 768, 1536, 1664, 1792, 1920, 2176, 2304, 2560)
IN_SIZES = (POOL_WIDTH, NSA_HEADS * HEAD_DIM, 6 * NSA_KV_HEADS * HEAD_DIM, 3 * NSA_HEADS,
            GLA_HEADS * GLA_DK, GLA_HEADS * GLA_DK, GLA_HEADS * GLA_DV, GLA_RANK, GLA_WIDTH)
IN_PADDED = (256, 512, 768, 128, 128, 128, 256, 128, 256)

VMEM_LIMIT = 56 * 1024 * 1024

NT_DIMS = (((1,), (1,)), ((), ()))
TN_DIMS = (((0,), (0,)), ((), ()))


def _params(sem):
    return pltpu.CompilerParams(dimension_semantics=sem, vmem_limit_bytes=VMEM_LIMIT)


def _bdot(a, b):
    return jnp.dot(a.astype(BF16), b.astype(BF16), preferred_element_type=F32)


def _rms(x, g):
    return x * lax.rsqrt(jnp.mean(x * x, axis=-1, keepdims=True) + EPS) * g


def _sigmoid(x):
    return 1.0 / (1.0 + jnp.exp(-x))


def _rope(xc, cs, sn, first_half):
    rot = jnp.where(first_half, pltpu.roll(xc, 96, 1), pltpu.roll(xc, 32, 1))
    return xc * cs + rot * sn


def _pre_kernel(x_ref, g_ref, w_ref, cs_ref, sn_ref, wa_ref, ba_ref,
                u_ref, q_ref, rows_ref, win_ref, kvd_ref, gates_ref,
                gq_ref, gk_ref, gv_ref, lg_ref, og_ref, kvc_ref):
    x = x_ref[...]
    tm = x.shape[0]
    h = _rms(x, g_ref[...])
    z = jnp.dot(h.astype(BF16), w_ref[...], preferred_element_type=F32)
    lane = lax.broadcasted_iota(jnp.int32, (tm, LANES), 1)
    first_half = (lane & 63) < 32
    low = lane < 64
    cs = cs_ref[...]
    sn = sn_ref[...]
    u_ref[...] = z[:, OFF_U:OFF_U + 256]
    for c in range(4):
        qc = z[:, OFF_Q + LANES * c:OFF_Q + LANES * (c + 1)]
        q_ref[:, LANES * c:LANES * (c + 1)] = _rope(qc, cs, sn, first_half) * (HEAD_DIM ** -0.5)
    planes = []
    for p in range(6):
        zc = z[:, OFF_KV + LANES * p:OFF_KV + LANES * (p + 1)]
        planes.append(_rope(zc, cs, sn, first_half) if p % 2 == 0 else zc)
    for p in range(4):
        rows_ref[:, LANES * p:LANES * (p + 1)] = planes[p]
    win_ref[:, 0:LANES] = planes[4]
    win_ref[:, LANES:2 * LANES] = planes[5]
    for j, p in enumerate((2, 3, 4, 5)):
        pc = planes[p]
        pr = pltpu.roll(pc, 64, 1)
        kvd_ref[:, 256 * j:256 * j + LANES] = jnp.where(low, pc, pr).astype(BF16)
        kvd_ref[:, 256 * j + LANES:256 * j + 256] = jnp.where(low, pr, pc).astype(BF16)
    gates_ref[...] = _sigmoid(z[:, OFF_NG:OFF_NG + LANES])
    gq_ref[...] = z[:, OFF_GQ:OFF_GQ + LANES] * (GLA_DK ** -0.5)
    gk_ref[...] = z[:, OFF_GK:OFF_GK + LANES]
    gv_ref[...] = z[:, OFF_GV:OFF_GV + 256]
    a = jnp.dot(z[:, OFF_LR:OFF_LR + LANES].astype(BF16), wa_ref[...], preferred_element_type=F32) + ba_ref[...]
    lg_ref[...] = (jnp.minimum(a, 0.0) - jnp.log(1.0 + jnp.exp(-jnp.abs(a)))) * (1.0 / GLA_NORMALIZER)
    og = z[:, OFF_OG:OFF_OG + 256]
    og_ref[...] = og * _sigmoid(og)
    nb = tm // NSA_BLOCK
    kvc_ref[:, 0:LANES] = jnp.sum(planes[0].reshape(nb, NSA_BLOCK, LANES), axis=1) * (1.0 / NSA_BLOCK)
    kvc_ref[:, LANES:2 * LANES] = jnp.sum(planes[1].reshape(nb, NSA_BLOCK, LANES), axis=1) * (1.0 / NSA_BLOCK)


def _pre_call(x, g0, w_in, cs, sn, wa, ba, tm):
    t, d = x.shape
    nt = t // tm
    ntab = cs.shape[0] // tm
    row = lambda w: pl.BlockSpec((tm, w), lambda i: (i, 0))
    const = lambda a: pl.BlockSpec(a.shape, lambda i: (0,) * a.ndim)
    tab = pl.BlockSpec((tm, LANES), lambda i: (i % ntab, 0))
    widths = (256, 512, 512, 256, 1024, 128, 128, 128, 256, 128, 256)
    dtypes = (F32, F32, F32, F32, BF16, F32, F32, F32, F32, F32, F32)
    out_shape = [jax.ShapeDtypeStruct((t, w), dt) for w, dt in zip(widths, dtypes)]
    out_shape.append(jax.ShapeDtypeStruct((t // NSA_BLOCK, 256), F32))
    out_specs = [row(w) for w in widths] + [pl.BlockSpec((tm // NSA_BLOCK, 256), lambda i: (i, 0))]
    return pl.pallas_call(
        _pre_kernel, grid=(nt,),
        in_specs=[row(d), const(g0), const(w_in), tab, tab, const(wa), const(ba)],
        out_specs=out_specs, out_shape=out_shape,
        compiler_params=_params(("arbitrary",)), name="pre",
    )(x, g0, w_in, cs, sn, wa, ba)


def _pool_select(lane_grp, sums):
    return jnp.where(lane_grp == 0, sums[0], jnp.where(lane_grp == 1, sums[1],
                     jnp.where(lane_grp == 2, sums[2], sums[3])))


def _pool_kernel(u_ref, w_ref, sc_ref, y_ref, carry_ref):
    i = pl.program_id(1)

    @pl.when(i == 0)
    def _():
        carry_ref[...] = jnp.zeros_like(carry_ref)

    u = u_ref[...]
    tl = u.shape[0]
    halo = carry_ref.shape[0]
    xx = jnp.concatenate([carry_ref[...], u], axis=0)
    carry_ref[...] = u[tl - halo:, :]
    w2 = xx + pltpu.roll(xx, 1, 0)
    w4 = w2 + pltpu.roll(w2, 2, 0)
    w8 = w4 + pltpu.roll(w4, 4, 0)
    w16 = w8 + pltpu.roll(w8, 8, 0)
    grp = lax.broadcasted_iota(jnp.int32, (tl, POOL_WIDTH), 1) >> 6
    pos = i * tl + lax.broadcasted_iota(jnp.int32, (tl, POOL_WIDTH), 0)
    wsum = _pool_select(grp, [w[halo:, :] for w in (w2, w4, w8, w16)])
    cnt = jnp.minimum(jnp.left_shift(2, grp), pos + 1).astype(F32)
    d = wsum / cnt - u
    y_ref[...] = _bdot(d, w_ref[...]) * sc_ref[...]


def _pool_call(u, wbd, scale, tl):
    b, l, c = u.shape
    return pl.pallas_call(
        _pool_kernel, grid=(b, l // tl),
        in_specs=[pl.BlockSpec((None, tl, c), lambda bi, i: (bi, i, 0)),
                  pl.BlockSpec(wbd.shape, lambda bi, i: (0, 0)),
                  pl.BlockSpec(scale.shape, lambda bi, i: (0, 0))],
        out_specs=pl.BlockSpec((None, tl, c), lambda bi, i: (bi, i, 0)),
        out_shape=jax.ShapeDtypeStruct((b, l, c), F32),
        scratch_shapes=[pltpu.VMEM((2 * SUBLANES, c), F32)],
        compiler_params=_params(("arbitrary", "arbitrary")), name="pool",
    )(u, wbd, scale)


def _pool_dec_kernel(full_ref, w_ref, sc_ref, y_ref, *, start):
    l, nb, c = y_ref.shape
    grp = lax.broadcasted_iota(jnp.int32, (nb, c), 1) >> 6
    for t in range(l):
        cur = full_ref[POOL_BUF + t]
        run = cur
        sums = []
        for k in range(1, max(POOL_WINDOWS)):
            run = run + full_ref[POOL_BUF + t - k]
            if k + 1 in POOL_WINDOWS:
                sums.append(run)
        wsum = _pool_select(grp, sums)
        cnt = jnp.minimum(jnp.left_shift(2, grp), start + t + 1).astype(F32)
        d = wsum / cnt - cur
        y_ref[t] = _bdot(d, w_ref[...]) * sc_ref[...]


def _pool_dec_call(full_t, wbd, scale, start):
    p_l, nb, c = full_t.shape
    l = p_l - POOL_BUF
    return pl.pallas_call(
        functools.partial(_pool_dec_kernel, start=start),
        out_shape=jax.ShapeDtypeStruct((l, nb, c), F32),
        compiler_params=pltpu.CompilerParams(vmem_limit_bytes=VMEM_LIMIT), name="pool_dec",
    )(full_t, wbd, scale)


def _gla_kernel(q_ref, k_ref, v_ref, lg_ref, og_ref, gn_ref, s0_ref, tri_ref, bm_ref, mavg_ref,
                y_ref, sf_ref, st_ref, *, chunk):
    i = pl.program_id(1)
    c = chunk

    @pl.when(i == 0)
    def _():
        st_ref[...] = s0_ref[...]

    tl = q_ref.shape[0]
    lane_k = lax.broadcasted_iota(jnp.int32, (c, GLA_HEADS * GLA_DK), 1) >> 5
    lane_v = lax.broadcasted_iota(jnp.int32, (c, GLA_WIDTH), 1) >> 6
    r4 = lax.broadcasted_iota(jnp.int32, (GLA_HEADS * c, c), 0) & (c - 1)
    c4 = lax.broadcasted_iota(jnp.int32, (GLA_HEADS * c, c), 1)
    tril4 = r4 >= c4
    for ci in range(tl // c):
        sl = slice(ci * c, (ci + 1) * c)
        qc, kc, vc = q_ref[sl, :], k_ref[sl, :], v_ref[sl, :]
        b = jnp.dot(tri_ref[...], lg_ref[sl, :], precision=HIGHEST, preferred_element_type=F32)
        bl = b[c - 1:c, :]
        qt = qc * jnp.exp(b)
        kt = kc * jnp.exp(-b)
        kd = kc * jnp.exp(bl - b)
        qst = jnp.concatenate([jnp.where(lane_k == h, qt, 0.0) for h in range(GLA_HEADS)], axis=0).astype(BF16)
        a = lax.dot_general(qst, kt.astype(BF16), NT_DIMS, preferred_element_type=F32)
        a = jnp.where(tril4, a, 0.0)
        of = _bdot(a, vc)
        st = st_ref[...]
        o = lax.dot_general(qt.astype(BF16), st.astype(BF16), NT_DIMS, preferred_element_type=F32)
        for h in range(GLA_HEADS):
            o = o + jnp.where(lane_v == h, of[h * c:(h + 1) * c, :], 0.0)
        upd = lax.dot_general(vc.astype(BF16), kd.astype(BF16), TN_DIMS, preferred_element_type=F32)
        st_ref[...] = st * jnp.exp(bl) + upd * bm_ref[...]
        msq = jnp.dot(o * o, mavg_ref[...], precision=HIGHEST, preferred_element_type=F32)
        y_ref[sl, :] = o * lax.rsqrt(msq + EPS) * gn_ref[...] * og_ref[sl, :]

    @pl.when(i == pl.num_programs(1) - 1)
    def _():
        sf_ref[...] = st_ref[...]


def _gla_consts(chunk):
    tri = np.tril(np.ones((chunk, chunk), np.float32))
    hv = np.arange(GLA_WIDTH) // GLA_DV
    hk = np.arange(GLA_HEADS * GLA_DK) // GLA_DK
    bm = (hv[:, None] == hk[None, :]).astype(np.float32)
    mavg = (hv[:, None] == hv[None, :]).astype(np.float32) / GLA_DV
    return jnp.asarray(tri), jnp.asarray(bm), jnp.asarray(mavg)


def _gla_call(gq, gk, gv, lg, og, gn, s0, tl, chunk):
    b, l, _ = gq.shape
    tri, bm, mavg = _gla_consts(chunk)
    seq = lambda w: pl.BlockSpec((None, tl, w), lambda bi, i: (bi, i, 0))
    const = lambda a: pl.BlockSpec(a.shape, lambda bi, i: (0,) * a.ndim)
    st_spec = pl.BlockSpec((None, GLA_WIDTH, GLA_HEADS * GLA_DK), lambda bi, i: (bi, 0, 0))
    return pl.pallas_call(
        functools.partial(_gla_kernel, chunk=chunk), grid=(b, l // tl),
        in_specs=[seq(128), seq(128), seq(256), seq(128), seq(256), const(gn), st_spec,
                  const(tri), const(bm), const(mavg)],
        out_specs=[seq(256), st_spec],
        out_shape=[jax.ShapeDtypeStruct((b, l, GLA_WIDTH), F32),
                   jax.ShapeDtypeStruct((b, GLA_WIDTH, GLA_HEADS * GLA_DK), F32)],
        scratch_shapes=[pltpu.VMEM((GLA_WIDTH, GLA_HEADS * GLA_DK), F32)],
        compiler_params=_params(("arbitrary", "arbitrary")), name="gla",
    )(gq, gk, gv, lg, og, gn, s0, tri, bm, mavg)


def _gla_state_to_bd(s):
    b = s.shape[0]
    sx = jnp.swapaxes(s, 2, 3)
    eye = jnp.eye(GLA_HEADS, dtype=s.dtype)
    bd = sx[:, :, :, None, :] * eye[None, :, None, :, None]
    return bd.reshape(b, GLA_WIDTH, GLA_HEADS * GLA_DK)


def _gla_state_from_bd(st):
    b = st.shape[0]
    r = st.reshape(b, GLA_HEADS, GLA_DV, GLA_HEADS, GLA_DK)
    d = jnp.stack([r[:, h, :, h, :] for h in range(GLA_HEADS)], axis=1)
    return jnp.swapaxes(d, 2, 3)


def _half_split(x, low):
    xr = pltpu.roll(x, 64, 1)
    lo = (jnp.where(low, x, 0.0), jnp.where(low, xr, 0.0))
    hi = (jnp.where(low, 0.0, xr), jnp.where(low, 0.0, x))
    return lo, hi


def _topk_mask(imp, n_idx, k):
    sel = jnp.zeros_like(imp)
    work = imp
    big = imp.shape[-1]
    for _ in range(k):
        mx = jnp.max(work, axis=-1, keepdims=True)
        idx = jnp.min(jnp.where(work == mx, n_idx, big), axis=-1, keepdims=True)
        idx = jnp.where(mx > NEG_INF, idx, -1)
        pick = n_idx == idx
        sel = jnp.where(pick, 1.0, sel)
        work = jnp.where(pick, NEG_INF, work)
    return sel


def _cmp_kernel(q_ref, kvc_ref, oc_ref, sel_ref, *, top_k):
    i = pl.program_id(1)
    tq = q_ref.shape[0]
    nbc = kvc_ref.shape[0]
    low = lax.broadcasted_iota(jnp.int32, (nbc, LANES), 1) < 64
    k_lo, k_hi = _half_split(kvc_ref[:, 0:LANES], low)
    v_lo, v_hi = _half_split(kvc_ref[:, LANES:2 * LANES], low)
    qpos = i * tq + lax.broadcasted_iota(jnp.int32, (tq, nbc), 0)
    n_idx = lax.broadcasted_iota(jnp.int32, (tq, nbc), 1)
    okc = (n_idx + 1) * NSA_BLOCK - 1 <= qpos
    cur = qpos >> 6
    oc = [jnp.zeros((tq, LANES), F32) for _ in range(4)]
    for g in range(NSA_KV_HEADS):
        imp = jnp.zeros((tq, nbc), F32)
        for hl in range(4):
            h = 4 * g + hl
            c = h // 2
            kx, vx = (k_lo[g], v_lo[g]) if h % 2 == 0 else (k_hi[g], v_hi[g])
            s = lax.dot_general(q_ref[:, LANES * c:LANES * (c + 1)], kx, NT_DIMS,
                                precision=HIGHEST, preferred_element_type=F32)
            s = jnp.where(okc, s, NEG_INF)
            m = jnp.max(s, axis=-1, keepdims=True)
            m = jnp.where(m == NEG_INF, 0.0, m)
            p = jnp.where(okc, jnp.exp(s - m), 0.0)
            p = p / jnp.maximum(jnp.sum(p, axis=-1, keepdims=True), 1e-30)
            imp = imp + p
            oc[c] = oc[c] + _bdot(p, vx)
        imp = jnp.where((n_idx == 0) | (n_idx == cur), NSA_FORCE, imp)
        imp = jnp.where(n_idx <= cur, imp, NEG_INF)
        sel = _topk_mask(imp, n_idx, top_k)
        sel_ref[:, g * nbc:(g + 1) * nbc] = sel.astype(BF16)
    for c in range(4):
        oc_ref[:, LANES * c:LANES * (c + 1)] = oc[c]


def _cmp_call(q, kvc, tq):
    b, l, _ = q.shape
    nbc = kvc.shape[1]
    return pl.pallas_call(
        functools.partial(_cmp_kernel, top_k=min(NSA_TOP_K, nbc)), grid=(b, l // tq),
        in_specs=[pl.BlockSpec((None, tq, NSA_WIDTH), lambda bi, i: (bi, i, 0)),
                  pl.BlockSpec((None, nbc, 256), lambda bi, i: (bi, 0, 0))],
        out_specs=[pl.BlockSpec((None, tq, NSA_WIDTH), lambda bi, i: (bi, i, 0)),
                   pl.BlockSpec((None, tq, 2 * nbc), lambda bi, i: (bi, i, 0))],
        out_shape=[jax.ShapeDtypeStruct((b, l, NSA_WIDTH), F32),
                   jax.ShapeDtypeStruct((b, l, 2 * nbc), BF16)],
        compiler_params=_params(("arbitrary", "arbitrary")), name="nsa_cmp",
    )(q, kvc)


def _flash_kernel(qi_ref, ki_ref, fl_ref, q_ref, k_ref, v_ref, *rest, tq, tk, selected):
    if selected:
        sel_ref, o_ref, qs_ref, m_ref, l_ref, acc_ref = rest
    else:
        o_ref, qs_ref, m_ref, l_ref, acc_ref = rest
    step = pl.program_id(2)
    qi = qi_ref[step]
    ki = ki_ref[step]
    flag = fl_ref[step]
    low = lax.broadcasted_iota(jnp.int32, (tq, LANES), 1) < 64

    @pl.when((flag & 1) == 1)
    def _():
        for c in range(2):
            qc = q_ref[:, LANES * c:LANES * (c + 1)]
            qs_ref[(2 * c) * tq:(2 * c + 1) * tq, :] = jnp.where(low, qc, 0.0).astype(BF16)
            qs_ref[(2 * c + 1) * tq:(2 * c + 2) * tq, :] = jnp.where(low, 0.0, qc).astype(BF16)
        m_ref[...] = jnp.full_like(m_ref, NEG_INF)
        l_ref[...] = jnp.zeros_like(l_ref)
        acc_ref[...] = jnp.zeros_like(acc_ref)

    s = lax.dot_general(qs_ref[...], k_ref[...], NT_DIMS, preferred_element_type=F32)
    qpos = qi * tq + lax.broadcasted_iota(jnp.int32, (tq, tk), 0)
    kpos = ki * tk + lax.broadcasted_iota(jnp.int32, (tq, tk), 1)
    if selected:
        nbs = sel_ref.shape[1]
        blk = (ki * tk + lax.broadcasted_iota(jnp.int32, (nbs, tk), 1)) >> 6
        expand = jnp.where(lax.broadcasted_iota(jnp.int32, (nbs, tk), 0) == blk, 1.0, 0.0).astype(BF16)
        picked = jnp.dot(sel_ref[...], expand, preferred_element_type=F32)
        bias = jnp.where(picked > 0.5, jnp.where(kpos <= qpos, 0.0, NEG_INF), NEG_INF)
    else:
        dist = qpos - kpos
        bias = jnp.where(dist >= 0, jnp.where(dist <= NSA_WINDOW, 0.0, NEG_INF), NEG_INF)
    s3 = s.reshape(4, tq, tk) + bias[None]
    m_prev = m_ref[...].reshape(4, tq, 1)
    m_new = jnp.maximum(m_prev, jnp.max(s3, axis=-1, keepdims=True))
    m_safe = jnp.where(m_new == NEG_INF, 0.0, m_new)
    alpha = jnp.exp(m_prev - m_safe)
    p = jnp.exp(s3 - m_safe)
    l_ref[...] = (alpha * l_ref[...].reshape(4, tq, 1) + jnp.sum(p, axis=-1, keepdims=True)).reshape(4 * tq, 1)
    pv = jnp.dot(p.reshape(4 * tq, tk).astype(BF16), v_ref[...], preferred_element_type=F32)
    acc_ref[...] = alpha.reshape(4 * tq, 1) * acc_ref[...] + pv
    m_ref[...] = m_new.reshape(4 * tq, 1)

    @pl.when((flag & 2) == 2)
    def _():
        o = acc_ref[...] / jnp.maximum(l_ref[...], 1e-30)
        for c in range(2):
            o_ref[:, LANES * c:LANES * (c + 1)] = jnp.where(
                low, o[(2 * c) * tq:(2 * c + 1) * tq, :], o[(2 * c + 1) * tq:(2 * c + 2) * tq, :])


def _flash_steps(nq, tq, tk, selected):
    qi, ki, fl = [], [], []
    for q in range(nq):
        k_hi = ((q + 1) * tq - 1) // tk
        k_lo = 0 if selected else max(0, (q * tq - NSA_WINDOW) // tk)
        for k in range(k_lo, k_hi + 1):
            qi.append(q)
            ki.append(k)
            fl.append((1 if k == k_lo else 0) | (2 if k == k_hi else 0))
    as_i32 = lambda v: jnp.asarray(np.asarray(v, np.int32))
    return as_i32(qi), as_i32(ki), as_i32(fl)


def _flash_call(q, kvd, sel, tq, tk, selected):
    b, l, _ = q.shape
    qi, ki, fl = _flash_steps(l // tq, tq, tk, selected)
    kcol = 0 if selected else 4
    in_specs = [pl.BlockSpec((None, tq, 256), lambda bi, g, s, qi, ki, fl: (bi, qi[s], g)),
                pl.BlockSpec((None, tk, LANES), lambda bi, g, s, qi, ki, fl: (bi, ki[s], kcol + g)),
                pl.BlockSpec((None, tk, LANES), lambda bi, g, s, qi, ki, fl: (bi, ki[s], kcol + 2 + g))]
    args = [q, kvd, kvd]
    if selected:
        nbs = sel.shape[2] // 2
        in_specs.append(pl.BlockSpec((None, tq, nbs), lambda bi, g, s, qi, ki, fl: (bi, qi[s], g)))
        args.append(sel)
    grid_spec = pltpu.PrefetchScalarGridSpec(
        num_scalar_prefetch=3, grid=(b, NSA_KV_HEADS, int(qi.shape[0])),
        in_specs=in_specs,
        out_specs=pl.BlockSpec((None, tq, 256), lambda bi, g, s, qi, ki, fl: (bi, qi[s], g)),
        scratch_shapes=[pltpu.VMEM((4 * tq, LANES), BF16), pltpu.VMEM((4 * tq, 1), F32),
                        pltpu.VMEM((4 * tq, 1), F32), pltpu.VMEM((4 * tq, LANES), F32)])
    return pl.pallas_call(
        functools.partial(_flash_kernel, tq=tq, tk=tk, selected=selected),
        grid_spec=grid_spec, out_shape=jax.ShapeDtypeStruct((b, l, NSA_WIDTH), F32),
        compiler_params=_params(("arbitrary", "arbitrary", "arbitrary")),
        name="nsa_sel" if selected else "nsa_win",
    )(qi, ki, fl, *args)


def _nsa_dec_kernel(pt_ref, q_ref, g_ref, new_ref, wnew_ref, *rest, n_pages, past, seq, top_k):
    pages = rest[:n_pages]
    win_ref, o_ref, wout_ref, kc_ref, ob_ref = rest[n_pages:]
    del pt_ref
    rows = q_ref.shape[0]
    page = pages[0].shape[0]
    nbp = page // NSA_BLOCK
    nb_past = n_pages * nbp
    q = q_ref[...]
    qb = q.astype(BF16)
    lane = lax.broadcasted_iota(jnp.int32, (rows, LANES), 1)
    tok = (lax.broadcasted_iota(jnp.int32, (rows, LANES), 0) >> 2) & (seq - 1)
    qpos = past + tok
    m_st = jnp.full((rows, LANES), NEG_INF, F32)
    l_st = jnp.zeros((rows, LANES), F32)
    kc_ref[...] = jnp.zeros_like(kc_ref)
    key_blk = lane >> 6
    for p in range(n_pages):
        pg = pages[p]
        kc_ref[nbp * p:nbp * (p + 1), :] = jnp.sum(
            pg[:, 0:256].reshape(nbp, NSA_BLOCK, 256), axis=1) * (1.0 / NSA_BLOCK)
        s = lax.dot_general(qb, pg[:, 256:384].astype(BF16), NT_DIMS, preferred_element_type=F32)
        vb = pg[:, 384:512].astype(BF16)
        mm = jnp.zeros_like(s)
        for j in range(nbp):
            mj = jnp.max(jnp.where(key_blk == j, s, NEG_INF), axis=-1, keepdims=True)
            mm = jnp.where(key_blk == j, mj, mm)
            m_st = jnp.where(lane == nbp * p + j, mj, m_st)
        pe = jnp.exp(s - mm)
        for j in range(nbp):
            pj = jnp.where(key_blk == j, pe, 0.0)
            l_st = jnp.where(lane == nbp * p + j, jnp.sum(pj, axis=-1, keepdims=True), l_st)
            ob_ref[nbp * p + j] = jnp.dot(pj.astype(BF16), vb, preferred_element_type=F32)
    nr = new_ref.shape[0]
    col = lax.broadcasted_iota(jnp.int32, (rows, nr), 1)
    tok_n = (lax.broadcasted_iota(jnp.int32, (rows, nr), 0) >> 2) & (seq - 1)
    ok_new = col <= tok_n
    sn = lax.dot_general(qb, new_ref[:, 256:384].astype(BF16), NT_DIMS, preferred_element_type=F32)
    sn = jnp.where(ok_new, sn, NEG_INF)
    mn = jnp.max(sn, axis=-1, keepdims=True)
    pn = jnp.where(ok_new, jnp.exp(sn - mn), 0.0)
    m_st = jnp.where(lane == nb_past, mn, m_st)
    l_st = jnp.where(lane == nb_past, jnp.sum(pn, axis=-1, keepdims=True), l_st)
    ob_ref[nb_past] = _bdot(pn, new_ref[:, 384:512])
    nbs = nb_past + 1
    nbc = (past + seq) // NSA_BLOCK
    kc = kc_ref[:, 0:LANES]
    vc = kc_ref[:, LANES:2 * LANES]
    sc = lax.dot_general(q, kc, NT_DIMS, precision=HIGHEST, preferred_element_type=F32)
    okc = ((lane + 1) * NSA_BLOCK - 1 <= qpos) & (lane < nbc)
    sc = jnp.where(okc, sc, NEG_INF)
    mc = jnp.max(sc, axis=-1, keepdims=True)
    mc = jnp.where(mc == NEG_INF, 0.0, mc)
    pc = jnp.where(okc, jnp.exp(sc - mc), 0.0)
    pc = pc / jnp.maximum(jnp.sum(pc, axis=-1, keepdims=True), 1e-30)
    o_c = _bdot(pc, vc)
    ri = lax.broadcasted_iota(jnp.int32, (rows, rows), 0) >> 2
    rj = lax.broadcasted_iota(jnp.int32, (rows, rows), 1) >> 2
    imp = jnp.dot(jnp.where(ri == rj, 1.0, 0.0), pc, precision=HIGHEST, preferred_element_type=F32)
    cur = qpos >> 6
    imp = jnp.where((lane == 0) | (lane == cur), NSA_FORCE, imp)
    imp = jnp.where((lane <= cur) & (lane < nbs), imp, NEG_INF)
    rank = jnp.zeros((rows, LANES), F32)
    for j in range(nbs):
        cj = imp[:, j:j + 1]
        rank = rank + jnp.where(cj > imp, 1.0, jnp.where(cj == imp, jnp.where(lane > j, 1.0, 0.0), 0.0))
    sel = jnp.where(rank < top_k, jnp.where(imp > NEG_INF, 1.0, 0.0), 0.0) > 0.5
    ms = jnp.max(jnp.where(sel, m_st, NEG_INF), axis=-1, keepdims=True)
    ms = jnp.where(ms == NEG_INF, 0.0, ms)
    e = jnp.where(sel, jnp.exp(m_st - ms), 0.0)
    l_s = jnp.sum(e * l_st, axis=-1, keepdims=True)
    o_s = jnp.zeros((rows, LANES), F32)
    for j in range(nbs):
        o_s = o_s + e[:, j:j + 1] * ob_ref[j]
    o_s = o_s / jnp.maximum(l_s, 1e-30)
    wb = win_ref.shape[0]
    sw = lax.dot_general(qb, win_ref[:, 0:LANES].astype(BF16), NT_DIMS, preferred_element_type=F32)
    wcol = lax.broadcasted_iota(jnp.int32, (rows, wb), 1)
    wtok = (lax.broadcasted_iota(jnp.int32, (rows, wb), 0) >> 2) & (seq - 1)
    wpos = past - wb + wcol
    dist = past + wtok - wpos
    ok_w = jnp.where(dist <= NSA_WINDOW, jnp.where(wpos >= 0, 1.0, 0.0), 0.0) > 0.5
    sw = jnp.where(ok_w, sw, NEG_INF)
    swn = lax.dot_general(qb, wnew_ref[:, 0:LANES].astype(BF16), NT_DIMS, preferred_element_type=F32)
    swn = jnp.where(ok_new, swn, NEG_INF)
    mw = jnp.maximum(jnp.max(sw, axis=-1, keepdims=True), jnp.max(swn, axis=-1, keepdims=True))
    pw = jnp.where(ok_w, jnp.exp(sw - mw), 0.0)
    pwn = jnp.where(ok_new, jnp.exp(swn - mw), 0.0)
    l_w = jnp.sum(pw, axis=-1, keepdims=True) + jnp.sum(pwn, axis=-1, keepdims=True)
    o_w = (_bdot(pw, win_ref[:, LANES:2 * LANES]) + _bdot(pwn, wnew_ref[:, LANES:2 * LANES])) / jnp.maximum(l_w, 1e-30)
    g = g_ref[...]
    o_ref[...] = g[:, 0:1] * o_c + g[:, 1:2] * o_s + g[:, 2:3] * o_w
    shifted = pltpu.roll(win_ref[...], wb - seq, 0)
    wout_ref[...] = shifted
    r8 = lax.broadcasted_iota(jnp.int32, (nr, 2 * LANES), 0)
    wout_ref[wb - nr:wb, :] = jnp.where(r8 >= nr - seq, pltpu.roll(wnew_ref[...], nr - seq, 0), shifted[wb - nr:wb, :])


def _nsa_dec_call(page_table, q32, g32, new8, wnew8, cache, win_buf, layer, n_phys, past, seq):
    nb, n_pages = page_table.shape
    page = cache.shape[1]
    wb = win_buf.shape[1]
    nr = new8.shape[1]
    rows = q32.shape[1]
    per_b = lambda shp: pl.BlockSpec((None,) + shp, lambda bi, pt: (bi,) + (0,) * len(shp))
    page_specs = [pl.BlockSpec((None, page, 512), functools.partial(
        lambda bi, pt, p: (layer * n_phys + pt[bi, p], 0, 0), p=p)) for p in range(n_pages)]
    grid_spec = pltpu.PrefetchScalarGridSpec(
        num_scalar_prefetch=1, grid=(nb,),
        in_specs=[per_b((rows, LANES)), per_b((rows, LANES)), per_b((nr, 512)), per_b((nr, 256))]
        + page_specs + [per_b((wb, 256))],
        out_specs=[per_b((rows, LANES)), per_b((wb, 256))],
        scratch_shapes=[pltpu.VMEM((LANES, 256), F32), pltpu.VMEM((LANES, rows, LANES), F32)])
    return pl.pallas_call(
        functools.partial(_nsa_dec_kernel, n_pages=n_pages, past=past, seq=seq,
                          top_k=min(NSA_TOP_K, n_pages * (page // NSA_BLOCK) + 1)),
        grid_spec=grid_spec,
        out_shape=[jax.ShapeDtypeStruct((nb, rows, LANES), F32), jax.ShapeDtypeStruct((nb, wb, 256), F32)],
        compiler_params=_params(("arbitrary",)), name="nsa_dec",
    )(page_table, q32, g32, new8, wnew8, *([cache] * n_pages), win_buf)


def _gelu_tanh(x):
    return 0.5 * x * (1.0 + jnp.tanh(0.7978845608028654 * (x + 0.044715 * (x * x * x))))


def _post_kernel(*refs, prompt, seq, d_ff, chunk):
    if prompt:
        (x_ref, yp_ref, oc_ref, os_ref, ow_ref, gt_ref, eg_ref, yg_ref, gn_ref, wo_ref, wu_ref, cw_ref, cb_ref,
         wd_ref, xo_ref, at_ref, tail_ref) = refs
    else:
        (x_ref, yp_ref, yn_ref, p1_ref, p2_ref, yg_ref, gn_ref, wo_ref, wu_ref, cw_ref, cb_ref,
         wd_ref, xo_ref, at_ref) = refs
    x = x_ref[...]
    tm = x.shape[0]
    if prompt:
        @pl.when(pl.program_id(1) == 0)
        def _():
            tail_ref[...] = jnp.zeros_like(tail_ref)

        gx = jnp.dot(gt_ref[...], eg_ref[...], precision=HIGHEST, preferred_element_type=F32)
        w = NSA_WIDTH
        y_nsa = gx[:, 0:w] * oc_ref[...] + gx[:, w:2 * w] * os_ref[...] + gx[:, 2 * w:3 * w] * ow_ref[...]
    else:
        y_nsa = yn_ref[...]
    mo = (_bdot(yp_ref[...], wo_ref[0:256, :]) + _bdot(y_nsa, wo_ref[256:768, :])
          + _bdot(yg_ref[...], wo_ref[768:1024, :]))
    x1 = x + _rms(mo, gn_ref[1:2, :])
    h2 = _rms(x1, gn_ref[2:3, :]).astype(BF16)
    row = lax.broadcasted_iota(jnp.int32, (tm, chunk), 0)
    tok = row if prompt else lax.rem(row, seq)
    acc = jnp.zeros((tm, x.shape[1]), F32)
    for c in range(d_ff // chunk):
        cs = slice(c * chunk, (c + 1) * chunk)
        a = jnp.dot(h2, wu_ref[:, cs], preferred_element_type=F32)
        bb = jnp.dot(h2, wu_ref[:, d_ff + c * chunk:d_ff + (c + 1) * chunk], preferred_element_type=F32)
        if prompt:
            t6 = tail_ref[6:7, cs]
            t7 = tail_ref[7:8, cs]
            am1 = jnp.where(tok == 0, t7, pltpu.roll(a, 1, 0))
            am2 = jnp.where(tok == 0, t6, jnp.where(tok == 1, t7, pltpu.roll(a, 2, 0)))
            tail_ref[:, cs] = a[tm - SUBLANES:, :]
            at_ref[:, cs] = a[tm - SUBLANES:, :]
        else:
            am1 = jnp.where(tok == 0, p1_ref[:, cs], pltpu.roll(a, 1, 0))
            am2 = jnp.where(tok < 2, p2_ref[:, cs], pltpu.roll(a, 2, 0))
            at_ref[:, cs] = a
        ac = cb_ref[:, cs] + ((cw_ref[0:1, cs] * am2 + cw_ref[1:2, cs] * am1) + cw_ref[2:3, cs] * a)
        acc = acc + _bdot(_gelu_tanh(ac) * bb, wd_ref[cs, :])
    xo_ref[...] = x1 + _rms(acc, gn_ref[3:4, :])


def _gate_expander():
    e = np.zeros((LANES, 3 * NSA_WIDTH), np.float32)
    for j in range(3):
        for h in range(NSA_HEADS):
            e[NSA_HEADS * j + h, NSA_WIDTH * j + HEAD_DIM * h:NSA_WIDTH * j + HEAD_DIM * (h + 1)] = 1.0
    return jnp.asarray(e)


def _single(shape, imap):
    return pl.BlockSpec(shape, imap, pipeline_mode=pl.Buffered(1))


def _post_prompt_call(x, yp, oc, osel, ow, gates, yg, gn, wo, wu, cw, cb, wd, tm, chunk):
    b, l, d = x.shape
    d_ff = wd.shape[0]
    eg = _gate_expander()
    seq = lambda w: pl.BlockSpec((None, tm, w), lambda bi, i: (bi, i, 0))
    const = lambda a: _single(a.shape, lambda bi, i: (0,) * a.ndim)
    return pl.pallas_call(
        functools.partial(_post_kernel, prompt=True, seq=l, d_ff=d_ff, chunk=chunk), grid=(b, l // tm),
        in_specs=[seq(d), seq(256), seq(512), seq(512), seq(512), seq(128), const(eg), seq(256),
                  const(gn), const(wo), const(wu), const(cw), const(cb), const(wd)],
        out_specs=[seq(d), pl.BlockSpec((None, SUBLANES, d_ff), lambda bi, i: (bi, 0, 0))],
        out_shape=[jax.ShapeDtypeStruct((b, l, d), F32), jax.ShapeDtypeStruct((b, SUBLANES, d_ff), F32)],
        scratch_shapes=[pltpu.VMEM((SUBLANES, d_ff), F32)],
        compiler_params=_params(("arbitrary", "arbitrary")), name="post",
    )(x, yp, oc, osel, ow, gates, eg, yg, gn, wo, wu, cw, cb, wd)


def _post_dec_call(x, yp, yn, p1, p2, yg, gn, wo, wu, cw, cb, wd, seq, tm, chunk):
    t, d = x.shape
    d_ff = wd.shape[0]
    row = lambda w: pl.BlockSpec((tm, w), lambda i: (i, 0))
    const = lambda a: _single(a.shape, lambda i: (0,) * a.ndim)
    return pl.pallas_call(
        functools.partial(_post_kernel, prompt=False, seq=seq, d_ff=d_ff, chunk=chunk), grid=(t // tm,),
        in_specs=[row(d), row(256), row(512), row(d_ff), row(d_ff), row(256),
                  const(gn), const(wo), const(wu), const(cw), const(cb), const(wd)],
        out_specs=[row(d), row(d_ff)],
        out_shape=[jax.ShapeDtypeStruct((t, d), F32), jax.ShapeDtypeStruct((t, d_ff), F32)],
        compiler_params=_params(("arbitrary",)), name="post_dec",
    )(x, yp, yn, p1, p2, yg, gn, wo, wu, cw, cb, wd)


def _pad_w_in(w_in):
    parts, o = [], 0
    for n, npad in zip(IN_SIZES, IN_PADDED):
        seg = w_in[..., o:o + n]
        if npad > n:
            seg = jnp.pad(seg, ((0, 0),) * (seg.ndim - 1) + ((0, npad - n),))
        parts.append(seg)
        o += n
    return jnp.concatenate(parts, axis=-1).astype(BF16)


def _rope_tables(pos):
    half = HEAD_DIM // 2
    inv = ROPE_THETA ** (-jnp.arange(half, dtype=F32) / half)
    ang = pos.astype(F32)[:, None] * inv[None, :]
    cos, sin = jnp.cos(ang), jnp.sin(ang)
    cs = jnp.concatenate([cos, cos, cos, cos], axis=-1)
    sn = jnp.concatenate([-sin, sin, -sin, sin], axis=-1)
    return cs, sn


def _pool_block_diag(pool_w):
    g, c, _ = pool_w.shape
    eye = jnp.eye(g, dtype=pool_w.dtype)
    return (pool_w[:, :, None, :] * eye[:, None, :, None]).reshape(g * c, g * c).astype(BF16)


def _tile(n, pref):
    t = min(n, pref)
    assert n % t == 0, (n, t)
    return t


def kernel(x_prompt, x_sample, cache_nsa_kv, state_nsa_win, state_pool, state_gla, state_ffn_conv, page_table,
           norm_gains, w_in, w_out, pool_w, pool_scale, gla_wa2, gla_ba, gla_norm, w_up, conv_w, conv_b, w_down):
    depth = w_in.shape[0]
    b, l, d = x_prompt.shape
    nb, ls, _ = x_sample.shape
    n_phys, page = cache_nsa_kv.shape[1], cache_nsa_kv.shape[2]
    n_pages = page_table.shape[1]
    past = n_pages * page
    wb = state_nsa_win.shape[2]
    d_ff = w_down.shape[1]
    assert d == D_MODEL and l % NSA_BLOCK == 0 and l >= POOL_BUF + 1
    assert past % NSA_BLOCK == 0 and 2 <= ls <= SUBLANES and (ls & (ls - 1)) == 0 and page == 2 * NSA_BLOCK
    assert n_pages * (page // NSA_BLOCK) + 1 <= LANES and wb % SUBLANES == 0

    w_in_p = _pad_w_in(w_in)
    w_out_b, w_up_b, w_down_b = w_out.astype(BF16), w_up.astype(BF16), w_down.astype(BF16)
    wa_p = jnp.pad(gla_wa2, ((0, 0), (0, LANES - GLA_RANK), (0, 0))).astype(BF16)
    cs_p, sn_p = _rope_tables(jnp.arange(l))
    pos_s = past + jnp.tile(jnp.arange(ls), nb)
    cs_s, sn_s = _rope_tables(pos_s)
    cache = cache_nsa_kv.reshape(depth * n_phys, page, 4 * NSA_KV_HEADS * HEAD_DIM)

    tm_p = _tile(b * l, 512)
    tl_pool = _tile(l, 512)
    tl_gla = _tile(l, 256)
    tq_cmp = _tile(l, 128)
    tq = _tile(l, 256)
    tm_post = _tile(l, 256)
    ts = nb * ls
    gla_chunk_s = SUBLANES
    ffn_chunk = 256
    assert d_ff % ffn_chunk == 0 and ts % SUBLANES == 0

    xp = x_prompt.reshape(b * l, d)
    xs = x_sample.reshape(ts, d)
    outs_p = [[] for _ in range(5)]
    outs_s = [[] for _ in range(5)]
    for li in range(depth):
        gn = norm_gains[li]
        g0 = gn[0:1]
        wbd = _pool_block_diag(pool_w[li])
        psc = pool_scale[li][None, :]
        ba = gla_ba[li][None, :]
        gnorm = gla_norm[li][None, :]
        cw, cb = conv_w[li], conv_b[li][None, :]

        u, q, rows, win, kvd, gates, gq, gk, gv, lg, og, kvc = _pre_call(
            xp, g0, w_in_p[li], cs_p, sn_p, wa_p[li], ba, tm_p)
        r3 = lambda a: a.reshape(b, l, a.shape[-1])
        y_pool = _pool_call(r3(u), wbd, psc, tl_pool)
        zero_state = jnp.zeros((b, GLA_WIDTH, GLA_HEADS * GLA_DK), F32)
        y_gla, s_fin = _gla_call(r3(gq), r3(gk), r3(gv), r3(lg), r3(og), gnorm, zero_state, tl_gla, GLA_CHUNK)
        o_c, sel = _cmp_call(r3(q), kvc.reshape(b, l // NSA_BLOCK, 256), tq_cmp)
        o_s = _flash_call(r3(q), r3(kvd), sel, tq, tq, True)
        o_w = _flash_call(r3(q), r3(kvd), None, tq, tq, False)
        xp3, a_tail = _post_prompt_call(xp.reshape(b, l, d), y_pool, o_c, o_s, o_w, r3(gates), y_gla, gn,
                                        w_out_b[li], w_up_b[li], cw, cb, w_down_b[li], tm_post, ffn_chunk)
        xp = xp3.reshape(b * l, d)
        outs_p[0].append(rows.reshape(b, l, 4, NSA_KV_HEADS, HEAD_DIM))
        win5 = win.reshape(b, l, 2, NSA_KV_HEADS, HEAD_DIM)
        outs_p[1].append(jnp.pad(win5, ((0, 0), (wb, 0), (0, 0), (0, 0), (0, 0)))[:, -wb:] if l < wb else win5[:, l - wb:])
        outs_p[2].append(r3(u)[:, l - POOL_BUF:])
        outs_p[3].append(_gla_state_from_bd(s_fin))
        outs_p[4].append(a_tail[:, SUBLANES - (CONV_WIDTH - 1):])

        u, q, rows, win, kvd, gates, gq, gk, gv, lg, og, kvc = _pre_call(
            xs, g0, w_in_p[li], cs_s, sn_s, wa_p[li], ba, ts)
        s3 = lambda a: a.reshape(nb, ls, a.shape[-1])
        full_t = jnp.concatenate([jnp.swapaxes(state_pool[li], 0, 1), jnp.swapaxes(s3(u), 0, 1)], axis=0)
        y_pool = jnp.swapaxes(_pool_dec_call(full_t, wbd, psc, past), 0, 1).reshape(ts, POOL_WIDTH)
        padc = lambda a: jnp.pad(s3(a), ((0, 0), (0, gla_chunk_s - ls), (0, 0)))
        y_gla, s_fin = _gla_call(padc(gq), padc(gk), padc(gv), padc(lg), padc(og), gnorm,
                                 _gla_state_to_bd(state_gla[li]), gla_chunk_s, gla_chunk_s)
        y_gla = y_gla[:, :ls].reshape(ts, GLA_WIDTH)
        q5 = q.reshape(nb, ls, NSA_KV_HEADS, 4, HEAD_DIM).transpose(0, 2, 1, 3, 4)
        eye_g = jnp.eye(NSA_KV_HEADS, dtype=F32)
        q32 = (q5[:, :, :, :, None, :] * eye_g[None, :, None, None, :, None]).reshape(nb, NSA_KV_HEADS * ls * 4, LANES)
        g5 = gates[:, :3 * NSA_HEADS].reshape(nb, ls, 3, NSA_KV_HEADS, 4).transpose(0, 3, 1, 4, 2)
        g32 = jnp.pad(g5.reshape(nb, NSA_KV_HEADS * ls * 4, 3), ((0, 0), (0, 0), (0, LANES - 3)))
        pad8 = lambda a: jnp.pad(s3(a), ((0, 0), (0, SUBLANES - ls), (0, 0)))
        o32, win_new = _nsa_dec_call(page_table, q32, g32, pad8(rows), pad8(win), cache,
                                     state_nsa_win[li].reshape(nb, wb, 256), li, n_phys, past, ls)
        o5 = o32.reshape(nb, NSA_KV_HEADS, ls, 4, NSA_KV_HEADS, HEAD_DIM)
        y_nsa = jnp.stack([o5[:, g, :, :, g, :] for g in range(NSA_KV_HEADS)], axis=2).reshape(ts, NSA_WIDTH)
        cbuf = state_ffn_conv[li]
        zrow = jnp.zeros((nb, ls - 1, d_ff), F32)
        p1 = jnp.concatenate([cbuf[:, 1:2], zrow], axis=1).reshape(ts, d_ff)
        p2 = jnp.concatenate([cbuf, zrow[:, 1:]], axis=1).reshape(ts, d_ff)
        xs, a_full = _post_dec_call(xs, y_pool, y_nsa, p1, p2, y_gla, gn, w_out_b[li], w_up_b[li], cw, cb,
                                    w_down_b[li], ls, ts, ffn_chunk)
        outs_s[0].append(rows.reshape(nb, ls, 4, NSA_KV_HEADS, HEAD_DIM))
        outs_s[1].append(win_new.reshape(nb, wb, 2, NSA_KV_HEADS, HEAD_DIM))
        outs_s[2].append(jnp.concatenate([state_pool[li], s3(u)], axis=1)[:, -POOL_BUF:])
        outs_s[3].append(_gla_state_from_bd(s_fin))
        outs_s[4].append(a_full.reshape(nb, ls, d_ff)[:, ls - (CONV_WIDTH - 1):])

    p_rows, p_win, p_pool, p_gla, p_conv = [jnp.stack(a) for a in outs_p]
    s_rows, s_win, s_pool, s_gla, s_conv = [jnp.stack(a) for a in outs_s]
    return (xp.reshape(b, l, d), xs.reshape(nb, ls, d), p_rows, s_rows, p_win, s_win, p_pool, s_pool,
            p_gla, s_gla, p_conv, s_conv)
```
